```python
import math
import jax, jax.numpy as jnp
from jax import lax
import numpy as np

D_MODEL = 1024
BATCH = 8
SEQ = 2048
DEPTH = 1

MEM_LEN = 256
RMS_EPS = 1e-6

SB_HEADS = 16
SB_HEAD_DIM = 64
SB_WIDTH = SB_HEADS * SB_HEAD_DIM
SB_BLOCK = 128

CONV_WIDTH = D_MODEL
CONV_K = 3

IN_COLS = 3 * SB_WIDTH + 3 * CONV_WIDTH + 2 * D_MODEL
IN_SPLITS = (SB_WIDTH, 2 * SB_WIDTH, 3 * SB_WIDTH,
             3 * SB_WIDTH + CONV_WIDTH, 3 * SB_WIDTH + 2 * CONV_WIDTH, 3 * SB_WIDTH + 3 * CONV_WIDTH,
             3 * SB_WIDTH + 3 * CONV_WIDTH + D_MODEL)

XATTN_HEADS = 4
XATTN_HEAD_DIM = D_MODEL // XATTN_HEADS

PEER_HEADS = 8
PEER_N_KEYS = 128
PEER_N_EXPERTS = PEER_N_KEYS * PEER_N_KEYS
PEER_KEY_DIM = 256
PEER_HALF = PEER_KEY_DIM // 2
PEER_TOPK = 16
PEER_TOKEN_CHUNK = 128

kernel_name = "hybrid_sb_shortconv_xmem_peer"


def rmsnorm(x, g):
    xf = x.astype(jnp.float32)
    y = xf * lax.rsqrt(jnp.mean(xf * xf, axis=-1, keepdims=True) + RMS_EPS) * g.astype(jnp.float32)
    return y.astype(x.dtype)


def stick_breaking_attention(q, k, v):
    s = q.shape[2]
    scale = 1.0 / math.sqrt(q.shape[-1])
    outs = []
    for blk in range(s // SB_BLOCK):
        start = blk * SB_BLOCK
        end = start + SB_BLOCK
        z = jnp.einsum('bhqd,bhkd->bhqk', q[:, :, start:end], k[:, :, :end]).astype(jnp.float32) * scale
        t_idx = start + jnp.arange(SB_BLOCK)[:, None]
        s_idx = jnp.arange(end)[None, :]
        causal = s_idx < t_idx
        log_stay = jnp.where(causal, jax.nn.log_sigmoid(-z), 0.0)
        log_after = lax.cumsum(log_stay, axis=3, reverse=True) - log_stay
        w = jnp.where(causal, jnp.exp(jax.nn.log_sigmoid(z) + log_after), 0.0)
        outs.append(jnp.einsum('bhqk,bhkd->bhqd', w.astype(v.dtype), v[:, :, :end]))
    return jnp.concatenate(outs, axis=2)


def causal_short_conv(u, w):
    s = u.shape[1]
    padded = jnp.pad(u, ((0, 0), (CONV_K - 1, 0), (0, 0)))
    y = padded[:, 0:s] * w[0]
    for i in range(1, CONV_K):
        y = y + padded[:, i:i + s] * w[i]
    return y


def memory_cross_attention(xn, memn, wq, wk, wv, wo):
    b, s, _ = xn.shape
    m = memn.shape[1]
    q = (xn @ wq).reshape(b, s, XATTN_HEADS, XATTN_HEAD_DIM)
    k = (memn @ wk).reshape(b, m, XATTN_HEADS, XATTN_HEAD_DIM)
    v = (memn @ wv).reshape(b, m, XATTN_HEADS, XATTN_HEAD_DIM)
    scores = jnp.einsum('bshd,bmhd->bhsm', q, k).astype(jnp.float32) / math.sqrt(XATTN_HEAD_DIM)
    p = jax.nn.softmax(scores, axis=-1).astype(v.dtype)
    o = jnp.einsum('bhsm,bmhd->bshd', p, v).reshape(b, s, XATTN_HEADS * XATTN_HEAD_DIM)
    return o @ wo


def peer_ffn(xn, w_query, sub_keys, expert_u, expert_v):
    b, s, d = xn.shape
    q = (xn @ w_query).reshape(b, s, PEER_HEADS, 2, PEER_HALF).astype(jnp.float32)
    sc = jnp.einsum('bshcd,hcnd->bshcn', q, sub_keys.astype(jnp.float32))
    top_s, top_i = lax.top_k(sc, PEER_TOPK)
    cand_s = top_s[..., 0, :, None] + top_s[..., 1, None, :]
    cand_i = top_i[..., 0, :, None] * PEER_N_KEYS + top_i[..., 1, None, :]
    cand_s = cand_s.reshape(b, s, PEER_HEADS, PEER_TOPK * PEER_TOPK)
    cand_i = cand_i.reshape(b, s, PEER_HEADS, PEER_TOPK * PEER_TOPK)
    best_s, best_pos = lax.top_k(cand_s, PEER_TOPK)
    expert_idx = jnp.take_along_axis(cand_i, best_pos, axis=-1)
    gates = jax.nn.softmax(best_s, axis=-1)

    n_tok = b * s
    n_chunks = n_tok // PEER_TOKEN_CHUNK
    n_sel = PEER_HEADS * PEER_TOPK
    xc = xn.reshape(n_chunks, PEER_TOKEN_CHUNK, d)
    ic = expert_idx.reshape(n_chunks, PEER_TOKEN_CHUNK, n_sel)
    gc = gates.reshape(n_chunks, PEER_TOKEN_CHUNK, n_sel).astype(xn.dtype)

    def chunk_fn(args):
        xb, ib, gb = args
        u = expert_u[ib]
        act = jax.nn.gelu(jnp.einsum('ced,cd->ce', u, xb))
        return jnp.einsum('ce,ced->cd', gb * act, expert_v[ib])

    out = lax.map(chunk_fn, (xc, ic, gc))
    return out.reshape(b, s, d)


def setup_inputs(seed: int = 0) -> dict:
    key = jax.random.key(seed)
    ks = jax.random.split(key, 24)

    def w(k, shape, fan_in):
        return jax.random.normal(k, shape, jnp.float32) * (fan_in ** -0.5)

    def gain(k, shape):
        return 1.0 + 0.02 * jax.random.normal(k, shape, jnp.float32)

    L = DEPTH
    return {
        "x": jax.random.normal(ks[0], (BATCH, SEQ, D_MODEL), jnp.float32),
        "mem": jax.random.normal(ks[1], (BATCH, MEM_LEN, D_MODEL), jnp.float32),
        "norm_mix": gain(ks[2], (L, D_MODEL)),
        "w_in": w(ks[3], (L, D_MODEL, IN_COLS), D_MODEL),
        "conv_w": w(ks[4], (L, CONV_K, CONV_WIDTH), CONV_K),
        "w_branch_attn": w(ks[5], (L, SB_WIDTH, D_MODEL), SB_WIDTH),
        "w_branch_conv": w(ks[6], (L, CONV_WIDTH, D_MODEL), CONV_WIDTH),
        "w_out": w(ks[7], (L, D_MODEL, D_MODEL), D_MODEL),
        "norm_xattn": gain(ks[8], (L, D_MODEL)),
        "norm_mem": gain(ks[9], (L, D_MODEL)),
        "xattn_wq": w(ks[10], (L, D_MODEL, XATTN_HEADS * XATTN_HEAD_DIM), D_MODEL),
        "xattn_wk": w(ks[11], (L, D_MODEL, XATTN_HEADS * XATTN_HEAD_DIM), D_MODEL),
        "xattn_wv": w(ks[12], (L, D_MODEL, XATTN_HEADS * XATTN_HEAD_DIM), D_MODEL),
        "xattn_wo": w(ks[13], (L, XATTN_HEADS * XATTN_HEAD_DIM, D_MODEL), XATTN_HEADS * XATTN_HEAD_DIM),
        "norm_ffn": gain(ks[14], (L, D_MODEL)),
        "peer_w_query": w(ks[15], (L, D_MODEL, PEER_HEADS * PEER_KEY_DIM), D_MODEL),
        "peer_sub_keys": w(ks[16], (L, PEER_HEADS, 2, PEER_N_KEYS, PEER_HALF), PEER_HALF),
        "peer_u": w(ks[17], (L, PEER_N_EXPERTS, D_MODEL), D_MODEL),
        "peer_v": w(ks[18], (L, PEER_N_EXPERTS, D_MODEL), PEER_HEADS),
        "norm_final": gain(ks[19], (D_MODEL,)),
    }


def reference(x, mem, norm_mix, w_in, conv_w, w_branch_attn, w_branch_conv, w_out,
              norm_xattn, norm_mem, xattn_wq, xattn_wk, xattn_wv, xattn_wo,
              norm_ffn, peer_w_query, peer_sub_keys, peer_u, peer_v, norm_final):
    b, s, _ = x.shape
    h = x
    for l in range(DEPTH):
        xn = rmsnorm(h, norm_mix[l])
        proj = xn @ w_in[l]
        q, k, v, cx, cb, cc, ga, gcv = jnp.split(proj, IN_SPLITS, axis=-1)

        def heads(t):
            return t.reshape(b, s, SB_HEADS, SB_HEAD_DIM).transpose(0, 2, 1, 3)

        y_attn = stick_breaking_attention(heads(q), heads(k), heads(v))
        y_attn = y_attn.transpose(0, 2, 1, 3).reshape(b, s, SB_WIDTH)
        y_conv = cb * causal_short_conv(cc * cx, conv_w[l])
        merged = (jax.nn.sigmoid(ga) * (y_attn @ w_branch_attn[l])
                  + jax.nn.sigmoid(gcv) * (y_conv @ w_branch_conv[l]))
        h = h + merged @ w_out[l]

        xn = rmsnorm(h, norm_xattn[l])
        memn = rmsnorm(mem, norm_mem[l])
        h = h + memory_cross_attention(xn, memn, xattn_wq[l], xattn_wk[l], xattn_wv[l], xattn_wo[l])

        xn = rmsnorm(h, norm_ffn[l])
        h = h + peer_ffn(xn, peer_w_query[l], peer_sub_keys[l], peer_u[l], peer_v[l])
    return rmsnorm(h, norm_final)
```

```python
import functools
import math

import jax
import jax.numpy as jnp
from jax import lax
from jax.experimental import pallas as pl
from jax.experimental.pallas import tpu as pltpu

RMS_EPS = 1e-6
SB_HEAD_DIM = 64
SB_HEADS_PER_STEP = 2
XATTN_HEADS = 4
PEER_HEADS = 8
PEER_N_KEYS = 128
PEER_TOPK = 16
LANES = 128
SUBLANES = 8
VMEM_LIMIT_BYTES = 48 * 1024 * 1024

F32 = jnp.float32
BF16 = jnp.bfloat16
NT_DIMS = (((1,), (1,)), ((), ()))


def _params(n_axes):
    return pltpu.CompilerParams(dimension_semantics=("arbitrary",) * n_axes,
                                vmem_limit_bytes=VMEM_LIMIT_BYTES)


def _rms(x, g):
    return x * lax.rsqrt(jnp.mean(x * x, axis=-1, keepdims=True) + RMS_EPS) * g


def _norm_matmul_kernel(x_ref, g_ref, w_ref, o_ref, xn_ref):
    @pl.when(pl.program_id(1) == 0)
    def _():
        xn_ref[...] = _rms(x_ref[...], g_ref[...]).astype(xn_ref.dtype)

    o_ref[...] = jnp.dot(xn_ref[...], w_ref[...], preferred_element_type=F32).astype(o_ref.dtype)


def _norm_matmul(x, g, w, out_dtype, tm, tn):
    m, k = x.shape
    n = w.shape[1]
    return pl.pallas_call(
        _norm_matmul_kernel,
        grid=(m // tm, n // tn),
        in_specs=[pl.BlockSpec((tm, k), lambda i, j: (i, 0)),
                  pl.BlockSpec((1, k), lambda i, j: (0, 0)),
                  pl.BlockSpec((k, tn), lambda i, j: (0, j))],
        out_specs=pl.BlockSpec((tm, tn), lambda i, j: (i, j)),
        out_shape=jax.ShapeDtypeStruct((m, n), out_dtype),
        scratch_shapes=[pltpu.VMEM((tm, k), BF16)],
        compiler_params=_params(2),
        name="norm_matmul",
    )(x, g.reshape(1, k), w)


def _sb_attn_kernel(q_ref, k_ref, v_ref, o_ref, *, blk):
    qi = pl.program_id(2)
    row = lax.broadcasted_iota(jnp.int32, (blk, blk), 0)
    col = lax.broadcasted_iota(jnp.int32, (blk, blk), 1)
    strict_lower = (row > col).astype(BF16)
    causal = col < row
    scale = 1.0 / math.sqrt(SB_HEAD_DIM)

    for h in range(SB_HEADS_PER_STEP):
        lanes = slice(h * SB_HEAD_DIM, (h + 1) * SB_HEAD_DIM)
        q = q_ref[0, :, lanes]

        def key_block(kb, carry, diagonal):
            acc, run = carry
            start = pl.multiple_of(kb * blk, blk)
            k = k_ref[0, pl.ds(start, blk), lanes]
            v = v_ref[0, pl.ds(start, blk), lanes]
            z = lax.dot_general(q, k, NT_DIMS, preferred_element_type=F32) * scale
            ls = -(jnp.maximum(z, 0.0) + jnp.log1p(jnp.exp(-jnp.abs(z))))
            ls_in = jnp.where(causal, ls, 0.0) if diagonal else ls
            hi = ls_in.astype(BF16)
            lo = (ls_in - hi.astype(F32)).astype(BF16)
            after = (jnp.dot(hi, strict_lower, preferred_element_type=F32)
                     + jnp.dot(lo, strict_lower, preferred_element_type=F32) + run)
            w = jnp.exp(z + ls + after)
            if diagonal:
                w = jnp.where(causal, w, 0.0)
            acc = acc + jnp.dot(w.astype(BF16), v, preferred_element_type=F32)
            run = run + jnp.sum(ls_in, axis=1, keepdims=True)
            return acc, run

        carry = (jnp.zeros((blk, SB_HEAD_DIM), F32), jnp.zeros((blk, 1), F32))
        carry = key_block(qi, carry, True)
        acc, _ = lax.fori_loop(0, qi, lambda i, c: key_block(qi - 1 - i, c, False), carry)
        o_ref[0, :, lanes] = acc.astype(o_ref.dtype)


def _sb_attention(qkv, n_heads, blk):
    b, s, _ = qkv.shape
    n_pairs = n_heads // SB_HEADS_PER_STEP
    width = SB_HEADS_PER_STEP * SB_HEAD_DIM
    return pl.pallas_call(
        functools.partial(_sb_attn_kernel, blk=blk),
        grid=(b, n_pairs, s // blk),
        in_specs=[pl.BlockSpec((1, blk, width), lambda bi, p, qi: (bi, qi, p)),
                  pl.BlockSpec((1, s, width), lambda bi, p, qi: (bi, 0, n_pairs + p)),
                  pl.BlockSpec((1, s, width), lambda bi, p, qi: (bi, 0, 2 * n_pairs + p))],
        out_specs=pl.BlockSpec((1, blk, width), lambda bi, p, qi: (bi, qi, p)),
        out_shape=jax.ShapeDtypeStruct((b, s, n_heads * SB_HEAD_DIM), BF16),
        compiler_params=_params(3),
        name="sb_attention",
    )(qkv, qkv, qkv)


def _mixer_out_kernel(cx_ref, cb_ref, cc_ref, ga_ref, gc_ref, ya_ref, x_ref, cw_ref,
                      wa_ref, wc_ref, wo_ref, o_ref, tail_ref):
    @pl.when(pl.program_id(1) == 0)
    def _():
        tail_ref[...] = jnp.zeros_like(tail_ref)

    u = cc_ref[...] * cx_ref[...]
    tm = u.shape[0]
    row = lax.broadcasted_iota(jnp.int32, u.shape, 0)
    prev1 = tail_ref[7:8, :]
    prev2 = tail_ref[6:7, :]
    u1 = jnp.where(row == 0, prev1, pltpu.roll(u, 1, axis=0))
    u2 = jnp.where(row == 0, prev2, jnp.where(row == 1, prev1, pltpu.roll(u, 2, axis=0)))
    conv = u2 * cw_ref[0:1, :] + u1 * cw_ref[1:2, :] + u * cw_ref[2:3, :]
    tail_ref[...] = u[tm - 8:, :]
    y_conv = cb_ref[...] * conv

    a = jnp.dot(ya_ref[...], wa_ref[...], preferred_element_type=F32)
    c = jnp.dot(y_conv.astype(BF16), wc_ref[...], preferred_element_type=F32)
    merged = jax.nn.sigmoid(ga_ref[...]) * a + jax.nn.sigmoid(gc_ref[...]) * c
    o_ref[...] = x_ref[...] + jnp.dot(merged.astype(BF16), wo_ref[...], preferred_element_type=F32)


def _mixer_out(rest, y_attn, x2d, conv_w, wa, wc, wo, batch, tm):
    n, d = x2d.shape
    tiles_per_seq = n // batch // tm
    tok = lambda bi, si: (bi * tiles_per_seq + si, 0)
    col = lambda c: (lambda bi, si: (bi * tiles_per_seq + si, c))
    const = lambda bi, si: (0, 0)
    return pl.pallas_call(
        _mixer_out_kernel,
        grid=(batch, tiles_per_seq),
        in_specs=[pl.BlockSpec((tm, d), col(c)) for c in range(5)]
                 + [pl.BlockSpec((tm, d), tok), pl.BlockSpec((tm, d), tok),
                    pl.BlockSpec(conv_w.shape, const),
                    pl.BlockSpec((d, d), const), pl.BlockSpec((d, d), const), pl.BlockSpec((d, d), const)],
        out_specs=pl.BlockSpec((tm, d), tok),
        out_shape=jax.ShapeDtypeStruct((n, d), F32),
        scratch_shapes=[pltpu.VMEM((8, d), F32)],
        compiler_params=_params(2),
        name="mixer_out",
    )(rest, rest, rest, rest, rest, y_attn, x2d, conv_w, wa, wc, wo)


def _xattn_kernel(h_ref, kv_ref, gx_ref, gf_ref, wq_ref, wo_ref, h2_ref, xn3_ref):
    h = h_ref[...]
    d = h.shape[1]
    hd = d // XATTN_HEADS
    xn = _rms(h, gx_ref[...]).astype(BF16)
    q = jnp.dot(xn, wq_ref[...], preferred_element_type=F32).astype(BF16)
    outs = []
    for i in range(XATTN_HEADS):
        k = kv_ref[0, :, i * hd:(i + 1) * hd]
        v = kv_ref[0, :, d + i * hd:d + (i + 1) * hd]
        s = lax.dot_general(q[:, i * hd:(i + 1) * hd], k, NT_DIMS, preferred_element_type=F32)
        s = s * (1.0 / math.sqrt(hd))
        p = jnp.exp(s - jnp.max(s, axis=-1, keepdims=True))
        p = p / jnp.sum(p, axis=-1, keepdims=True)
        outs.append(jnp.dot(p.astype(BF16), v, preferred_element_type=F32).astype(BF16))
    o = jnp.concatenate(outs, axis=1)
    h2 = h + jnp.dot(o, wo_ref[...], preferred_element_type=F32)
    h2_ref[...] = h2
    xn3_ref[...] = _rms(h2, gf_ref[...]).astype(BF16)


def _cross_attention(h1, kv, g_x, g_ffn, wq, wo, batch, tm):
    n, d = h1.shape
    tiles_per_seq = n // batch // tm
    m = kv.shape[1]
    tok = lambda bi, si: (bi * tiles_per_seq + si, 0)
    const = lambda bi, si: (0, 0)
    return pl.pallas_call(
        _xattn_kernel,
        grid=(batch, tiles_per_seq),
        in_specs=[pl.BlockSpec((tm, d), tok),
                  pl.BlockSpec((1, m, 2 * d), lambda bi, si: (bi, 0, 0)),
                  pl.BlockSpec((1, d), const), pl.BlockSpec((1, d), const),
                  pl.BlockSpec((d, d), const), pl.BlockSpec((d, d), const)],
        out_specs=[pl.BlockSpec((tm, d), tok), pl.BlockSpec((tm, d), tok)],
        out_shape=[jax.ShapeDtypeStruct((n, d), F32), jax.ShapeDtypeStruct((n, d), BF16)],
        compiler_params=_params(2),
        name="cross_attention",
    )(h1, kv, g_x.reshape(1, d), g_ffn.reshape(1, d), wq, wo)


def _top_ranks(scores, n_rounds):
    n = scores.shape[0]
    idx = lax.broadcasted_iota(jnp.int32, scores.shape, 0).astype(F32)
    work = scores
    rank = jnp.full(scores.shape, float(n_rounds), F32)
    tops = []
    for r in range(n_rounds):
        m = jnp.max(work, axis=0, keepdims=True)
        first = jnp.min(jnp.where(work == m, idx, float(n)), axis=0, keepdims=True)
        sel = idx == first
        rank = jnp.where(sel, float(r), rank)
        work = jnp.where(sel, -jnp.inf, work)
        tops.append(m)
    return rank, jnp.concatenate(tops, axis=0)


def _peer_route_kernel(xn_ref, wq_ref, keys_ref, r2_ref, e2_ref, lim_ref, e1_ref, q_scr):
    tile = xn_ref.shape[0]
    kd = LANES
    q = jnp.dot(xn_ref[...], wq_ref[...], preferred_element_type=F32).astype(BF16)
    for hc in range(2 * PEER_HEADS):
        q_scr[hc] = q[:, hc * kd:(hc + 1) * kd]

    k = PEER_TOPK
    for ts in range(tile // LANES):
        tok = slice(ts * LANES, (ts + 1) * LANES)

        def head(h, _):
            def sub_scores(c):
                return lax.dot_general(keys_ref[2 * h + c], q_scr[2 * h + c, tok, :], NT_DIMS,
                                       preferred_element_type=F32)

            rank1, t1 = _top_ranks(sub_scores(0), k)
            rank2, t2 = _top_ranks(sub_scores(1), k)
            cand = (t1[:, None, :] + t2[None, :, :]).reshape(k * k, LANES)
            sel_rank, _ = _top_ranks(cand, k)
            sel = (sel_rank < float(k)).astype(F32).reshape(k, k, LANES)
            row_len = jnp.sum(sel, axis=1)
            g1 = jnp.exp(t1 - t1[0:1, :])
            g2 = jnp.exp(t2 - t2[0:1, :])
            z = jnp.sum(jnp.sum(sel * g2[None, :, :], axis=1) * g1, axis=0, keepdims=True)
            lim = jnp.zeros((PEER_N_KEYS, LANES), F32)
            for i in range(k):
                lim = jnp.where(rank1 == float(i), row_len[i:i + 1, :], lim)
            lim_ref[h, :, tok] = lim
            e1_ref[h, :, tok] = jnp.exp(sub_scores(0) - t1[0:1, :]) / z
            e2_ref[h, :, tok] = jnp.exp(sub_scores(1) - t2[0:1, :])
            r2_ref[h, :, tok] = rank2
            return 0

        lax.fori_loop(0, PEER_HEADS, head, 0)


def _peer_route(xn3, w_query, keys, tile):
    n, d = xn3.shape
    table = jax.ShapeDtypeStruct((PEER_HEADS, PEER_N_KEYS, n), F32)
    table_spec = pl.BlockSpec((PEER_HEADS, PEER_N_KEYS, tile), lambda i: (0, 0, i))
    return pl.pallas_call(
        _peer_route_kernel,
        grid=(n // tile,),
        in_specs=[pl.BlockSpec((tile, d), lambda i: (i, 0)),
                  pl.BlockSpec(w_query.shape, lambda i: (0, 0)),
                  pl.BlockSpec(keys.shape, lambda i: (0, 0, 0))],
        out_specs=[table_spec] * 4,
        out_shape=[table] * 4,
        scratch_shapes=[pltpu.VMEM((2 * PEER_HEADS, tile, LANES), BF16)],
        compiler_params=_params(1),
        name="peer_route",
    )(xn3, w_query, keys)


def _gelu_tanh(x):
    return 0.5 * x * (1.0 + jnp.tanh(math.sqrt(2.0 / math.pi) * (x + 0.044715 * (x * x * x))))


def _peer_dense_kernel(xn_ref, h2_ref, u_ref, vt_ref, r2_ref, e2_ref, lim_ref, e1_ref, gf_ref,
                       o_ref, act_scr, p_scr, acc_scr):
    j = pl.program_id(1)
    n_exp = u_ref.shape[0]
    tile = xn_ref.shape[0]
    groups = n_exp // PEER_N_KEYS
    assert groups == SUBLANES

    @pl.when(j == 0)
    def _():
        acc_scr[...] = jnp.zeros_like(acc_scr)

    act_scr[...] = lax.dot_general(u_ref[...], xn_ref[...], NT_DIMS, preferred_element_type=F32)

    a_rows = pl.ds(pl.multiple_of(j * groups, SUBLANES), SUBLANES)
    for ts in range(tile // LANES):
        tok = slice(ts * LANES, (ts + 1) * LANES)
        lim8 = [lim_ref[h, a_rows, tok] for h in range(PEER_HEADS)]
        e18 = [e1_ref[h, a_rows, tok] for h in range(PEER_HEADS)]
        for g in range(groups):
            rows = slice(g * PEER_N_KEYS, (g + 1) * PEER_N_KEYS)
            gate = jnp.zeros((PEER_N_KEYS, LANES), F32)
            for h in range(PEER_HEADS):
                keep = r2_ref[h, :, tok] < lim8[h][g:g + 1, :]
                gate = gate + jnp.where(keep, e2_ref[h, :, tok] * e18[h][g:g + 1, :], 0.0)
            p_scr[rows, tok] = (gate * _gelu_tanh(act_scr[rows, tok])).astype(BF16)
    acc_scr[...] += jnp.dot(vt_ref[...], p_scr[...], preferred_element_type=F32)

    @pl.when(j == pl.num_programs(1) - 1)
    def _():
        h3 = h2_ref[...] + acc_scr[...].T
        o_ref[...] = _rms(h3, gf_ref[...])


def _peer_dense(xn3, h2, u, vt, tables, g_final, tile, n_exp_blk):
    n, d = xn3.shape
    n_experts = u.shape[0]
    table_spec = pl.BlockSpec((PEER_HEADS, PEER_N_KEYS, tile), lambda i, j: (0, 0, i))
    return pl.pallas_call(
        _peer_dense_kernel,
        grid=(n // tile, n_experts // n_exp_blk),
        in_specs=[pl.BlockSpec((tile, d), lambda i, j: (i, 0)),
                  pl.BlockSpec((tile, d), lambda i, j: (i, 0)),
                  pl.BlockSpec((n_exp_blk, d), lambda i, j: (j, 0)),
                  pl.BlockSpec((d, n_exp_blk), lambda i, j: (0, j))]
                 + [table_spec] * 4
                 + [pl.BlockSpec((1, d), lambda i, j: (0, 0))],
        out_specs=pl.BlockSpec((tile, d), lambda i, j: (i, 0)),
        out_shape=jax.ShapeDtypeStruct((n, d), F32),
        scratch_shapes=[pltpu.VMEM((n_exp_blk, tile), F32),
                        pltpu.VMEM((n_exp_blk, tile), BF16),
                        pltpu.VMEM((d, tile), F32)],
        compiler_params=_params(2),
        name="peer_dense",
    )(xn3, h2, u, vt, *tables, g_final.reshape(1, d))


def _tile_sizes(n_tokens, seq):
    return dict(
        proj_tm=min(1024, n_tokens), proj_tn=1024,
        sb_blk=min(256, seq),
        mixer_tm=min(256, seq),
        xattn_tm=min(512, seq),
        route_tile=min(256, n_tokens),
        dense_tile=min(512, n_tokens), dense_experts=1024,
    )


def kernel(x, mem, norm_mix, w_in, conv_w, w_branch_attn, w_branch_conv, w_out, norm_xattn, norm_mem,
           xattn_wq, xattn_wk, xattn_wv, xattn_wo, norm_ffn, peer_w_query, peer_sub_keys, peer_u, peer_v,
           norm_final):
    b, s, d = x.shape
    n = b * s
    assert norm_mix.shape[0] == 1, "single-layer block: the final RMSNorm is fused into the last kernel"
    n_sb_heads = w_branch_attn.shape[1] // SB_HEAD_DIM
    sb_width = n_sb_heads * SB_HEAD_DIM
    ts = _tile_sizes(n, s)
    h = x.reshape(n, d)
    mem2d = mem.reshape(-1, d)

    w_in_bf = w_in[0].astype(BF16)
    qkv = _norm_matmul(h, norm_mix[0], w_in_bf[:, :3 * sb_width], BF16, ts["proj_tm"], ts["proj_tn"])
    rest = _norm_matmul(h, norm_mix[0], w_in_bf[:, 3 * sb_width:], F32, ts["proj_tm"], ts["proj_tn"])
    y_attn = _sb_attention(qkv.reshape(b, s, 3 * sb_width), n_sb_heads, ts["sb_blk"])
    h1 = _mixer_out(rest, y_attn.reshape(n, sb_width), h, conv_w[0],
                    w_branch_attn[0].astype(BF16), w_branch_conv[0].astype(BF16), w_out[0].astype(BF16),
                    b, ts["mixer_tm"])

    w_kv = jnp.concatenate([xattn_wk[0], xattn_wv[0]], axis=1).astype(BF16)
    kv = _norm_matmul(mem2d, norm_mem[0], w_kv, BF16, min(1024, mem2d.shape[0]), ts["proj_tn"])
    h2, xn3 = _cross_attention(h1, kv.reshape(b, -1, 2 * d), norm_xattn[0], norm_ffn[0],
                               xattn_wq[0].astype(BF16), xattn_wo[0].astype(BF16), b, ts["xattn_tm"])

    keys = peer_sub_keys[0].reshape(2 * PEER_HEADS, PEER_N_KEYS, -1).astype(BF16)
    tables = _peer_route(xn3, peer_w_query[0].astype(BF16), keys, ts["route_tile"])
    out = _peer_dense(xn3, h2, peer_u[0].astype(BF16), peer_v[0].astype(BF16).T, tables, norm_final,
                      ts["dense_tile"], ts["dense_experts"])
    return out.reshape(b, s, d)
```

```python
import functools
import math

import jax
import jax.numpy as jnp
from jax import lax
from jax.experimental import pallas as pl
from jax.experimental.pallas import tpu as pltpu

RMS_EPS = 1e-6
SB_HEAD_DIM = 64
SB_HEADS_PER_STEP = 2
XATTN_HEADS = 4
PEER_HEADS = 8
PEER_N_KEYS = 128
PEER_TOPK = 16
LANES = 128
SUBLANES = 8
VMEM_LIMIT_BYTES = 48 * 1024 * 1024

F32 = jnp.float32
BF16 = jnp.bfloat16
NT_DIMS = (((1,), (1,)), ((), ()))
LOG2_E = 1.4426950408889634


def _params(n_axes):
    return pltpu.CompilerParams(dimension_semantics=("arbitrary",) * n_axes,
                                vmem_limit_bytes=VMEM_LIMIT_BYTES)


def _rms(x, g):
    return x * lax.rsqrt(jnp.mean(x * x, axis=-1, keepdims=True) + RMS_EPS) * g


def _norm_matmul_kernel(x_ref, g_ref, w_ref, o_ref, xn_ref):
    @pl.when(pl.program_id(1) == 0)
    def _():
        xn_ref[...] = _rms(x_ref[...], g_ref[...]).astype(xn_ref.dtype)

    o_ref[...] = jnp.dot(xn_ref[...], w_ref[...], preferred_element_type=F32).astype(o_ref.dtype)


def _norm_matmul(x, g, w, out_dtype, tm, tn):
    m, k = x.shape
    n = w.shape[1]
    return pl.pallas_call(
        _norm_matmul_kernel,
        grid=(m // tm, n // tn),
        in_specs=[pl.BlockSpec((tm, k), lambda i, j: (i, 0)),
                  pl.BlockSpec((1, k), lambda i, j: (0, 0)),
                  pl.BlockSpec((k, tn), lambda i, j: (0, j))],
        out_specs=pl.BlockSpec((tm, tn), lambda i, j: (i, j)),
        out_shape=jax.ShapeDtypeStruct((m, n), out_dtype),
        scratch_shapes=[pltpu.VMEM((tm, k), BF16)],
        compiler_params=_params(2),
        name="norm_matmul",
    )(x, g.reshape(1, k), w)


def _sb_attn_kernel(q_ref, k_ref, v_ref, tri_ref, o_ref, q2_scr, *, blk_q, blk_k):
    qi = pl.program_id(2)
    n_kb = blk_q // blk_k
    rows2 = 2 * blk_q
    lane = lax.broadcasted_iota(jnp.int32, (blk_q, LANES), 1)
    first_head = lane < SB_HEAD_DIM

    q = (q_ref[0].astype(F32) * (LOG2_E / math.sqrt(SB_HEAD_DIM))).astype(BF16)
    zero = jnp.zeros_like(q)
    q2_scr[0:blk_q, :] = jnp.where(first_head, q, zero)
    q2_scr[blk_q:rows2, :] = jnp.where(first_head, zero, q)

    def key_block(kb, carry, masked):
        acc, run = carry
        start = pl.multiple_of(kb * blk_k, blk_k)
        k = k_ref[0, pl.ds(start, blk_k), :]
        v = v_ref[0, pl.ds(start, blk_k), :]
        z = lax.dot_general(q2_scr[...], k, NT_DIMS, preferred_element_type=F32)
        soft = jnp.log2(1.0 + jnp.exp2(-jnp.abs(z)))
        ls = -jnp.maximum(z, 0.0) - soft
        lw = z + ls
        if masked:
            t_idx = qi * blk_q + lax.rem(lax.broadcasted_iota(jnp.int32, z.shape, 0), blk_q)
            s_idx = start + lax.broadcasted_iota(jnp.int32, z.shape, 1)
            causal = s_idx < t_idx
            ls = jnp.where(causal, ls, 0.0)
        hi = ls.astype(BF16)
        lo = (ls - hi.astype(F32)).astype(BF16)
        part = jnp.dot(jnp.concatenate([hi, lo], axis=0), tri_ref[...], preferred_element_type=F32)
        after = part[0:rows2] + part[rows2:2 * rows2] + run
        w = jnp.exp2(lw + after)
        if masked:
            w = jnp.where(causal, w, 0.0)
        acc = acc + jnp.dot(w.astype(BF16), v, preferred_element_type=F32)
        run = run + jnp.sum(ls, axis=1, keepdims=True)
        return acc, run

    carry = (jnp.zeros((rows2, LANES), F32), jnp.zeros((rows2, 1), F32))
    last = (qi + 1) * n_kb - 1
    for d in range(n_kb):
        carry = key_block(last - d, carry, True)
    acc, _ = lax.fori_loop(0, qi * n_kb, lambda i, c: key_block(qi * n_kb - 1 - i, c, False), carry)
    o_ref[0] = jnp.where(first_head, acc[0:blk_q], acc[blk_q:rows2]).astype(o_ref.dtype)


def _sb_attention(qkv, n_heads, blk_q, blk_k):
    b, s, _ = qkv.shape
    n_pairs = n_heads // SB_HEADS_PER_STEP
    width = SB_HEADS_PER_STEP * SB_HEAD_DIM
    assert width == LANES
    row = lax.broadcasted_iota(jnp.int32, (blk_k, blk_k), 0)
    col = lax.broadcasted_iota(jnp.int32, (blk_k, blk_k), 1)
    strict_lower = (row > col).astype(BF16)
    return pl.pallas_call(
        functools.partial(_sb_attn_kernel, blk_q=blk_q, blk_k=blk_k),
        grid=(b, n_pairs, s // blk_q),
        in_specs=[pl.BlockSpec((1, blk_q, width), lambda bi, p, qi: (bi, qi, p)),
                  pl.BlockSpec((1, s, width), lambda bi, p, qi: (bi, 0, n_pairs + p)),
                  pl.BlockSpec((1, s, width), lambda bi, p, qi: (bi, 0, 2 * n_pairs + p)),
                  pl.BlockSpec((blk_k, blk_k), lambda bi, p, qi: (0, 0))],
        out_specs=pl.BlockSpec((1, blk_q, width), lambda bi, p, qi: (bi, qi, p)),
        out_shape=jax.ShapeDtypeStruct((b, s, n_heads * SB_HEAD_DIM), BF16),
        scratch_shapes=[pltpu.VMEM((2 * blk_q, width), BF16)],
        compiler_params=_params(3),
        name="sb_attention",
    )(qkv, qkv, qkv, strict_lower)


def _mixer_out_kernel(cx_ref, cb_ref, cc_ref, ga_ref, gc_ref, ya_ref, x_ref, cw_ref,
                      wa_ref, wc_ref, wo_ref, o_ref, tail_ref):
    @pl.when(pl.program_id(1) == 0)
    def _():
        tail_ref[...] = jnp.zeros_like(tail_ref)

    u = cc_ref[...] * cx_ref[...]
    tm = u.shape[0]
    row = lax.broadcasted_iota(jnp.int32, u.shape, 0)
    prev1 = tail_ref[7:8, :]
    prev2 = tail_ref[6:7, :]
    u1 = jnp.where(row == 0, prev1, pltpu.roll(u, 1, axis=0))
    u2 = jnp.where(row == 0, prev2, jnp.where(row == 1, prev1, pltpu.roll(u, 2, axis=0)))
    conv = u2 * cw_ref[0:1, :] + u1 * cw_ref[1:2, :] + u * cw_ref[2:3, :]
    tail_ref[...] = u[tm - 8:, :]
    y_conv = cb_ref[...] * conv

    a = jnp.dot(ya_ref[...], wa_ref[...], preferred_element_type=F32)
    c = jnp.dot(y_conv.astype(BF16), wc_ref[...], preferred_element_type=F32)
    merged = jax.nn.sigmoid(ga_ref[...]) * a + jax.nn.sigmoid(gc_ref[...]) * c
    o_ref[...] = x_ref[...] + jnp.dot(merged.astype(BF16), wo_ref[...], preferred_element_type=F32)


def _mixer_out(rest, y_attn, x2d, conv_w, wa, wc, wo, batch, tm):
    n, d = x2d.shape
    tiles_per_seq = n // batch // tm
    tok = lambda bi, si: (bi * tiles_per_seq + si, 0)
    col = lambda c: (lambda bi, si: (bi * tiles_per_seq + si, c))
    const = lambda bi, si: (0, 0)
    return pl.pallas_call(
        _mixer_out_kernel,
        grid=(batch, tiles_per_seq),
        in_specs=[pl.BlockSpec((tm, d), col(c)) for c in range(5)]
                 + [pl.BlockSpec((tm, d), tok), pl.BlockSpec((tm, d), tok),
                    pl.BlockSpec(conv_w.shape, const),
                    pl.BlockSpec((d, d), const), pl.BlockSpec((d, d), const), pl.BlockSpec((d, d), const)],
        out_specs=pl.BlockSpec((tm, d), tok),
        out_shape=jax.ShapeDtypeStruct((n, d), F32),
        scratch_shapes=[pltpu.VMEM((8, d), F32)],
        compiler_params=_params(2),
        name="mixer_out",
    )(rest, rest, rest, rest, rest, y_attn, x2d, conv_w, wa, wc, wo)


def _xattn_kernel(h_ref, kv_ref, gx_ref, gf_ref, wq_ref, wo_ref, h2_ref, xn3_ref):
    h = h_ref[...]
    d = h.shape[1]
    hd = d // XATTN_HEADS
    xn = _rms(h, gx_ref[...]).astype(BF16)
    q = jnp.dot(xn, wq_ref[...], preferred_element_type=F32).astype(BF16)
    outs = []
    for i in range(XATTN_HEADS):
        k = kv_ref[0, :, i * hd:(i + 1) * hd]
        v = kv_ref[0, :, d + i * hd:d + (i + 1) * hd]
        s = lax.dot_general(q[:, i * hd:(i + 1) * hd], k, NT_DIMS, preferred_element_type=F32)
        s = s * (1.0 / math.sqrt(hd))
        p = jnp.exp(s - jnp.max(s, axis=-1, keepdims=True))
        p = p / jnp.sum(p, axis=-1, keepdims=True)
        outs.append(jnp.dot(p.astype(BF16), v, preferred_element_type=F32).astype(BF16))
    o = jnp.concatenate(outs, axis=1)
    h2 = h + jnp.dot(o, wo_ref[...], preferred_element_type=F32)
    h2_ref[...] = h2
    xn3_ref[...] = _rms(h2, gf_ref[...]).astype(BF16)


def _cross_attention(h1, kv, g_x, g_ffn, wq, wo, batch, tm):
    n, d = h1.shape
    tiles_per_seq = n // batch // tm
    m = kv.shape[1]
    tok = lambda bi, si: (bi * tiles_per_seq + si, 0)
    const = lambda bi, si: (0, 0)
    return pl.pallas_call(
        _xattn_kernel,
        grid=(batch, tiles_per_seq),
        in_specs=[pl.BlockSpec((tm, d), tok),
                  pl.BlockSpec((1, m, 2 * d), lambda bi, si: (bi, 0, 0)),
                  pl.BlockSpec((1, d), const), pl.BlockSpec((1, d), const),
                  pl.BlockSpec((d, d), const), pl.BlockSpec((d, d), const)],
        out_specs=[pl.BlockSpec((tm, d), tok), pl.BlockSpec((tm, d), tok)],
        out_shape=[jax.ShapeDtypeStruct((n, d), F32), jax.ShapeDtypeStruct((n, d), BF16)],
        compiler_params=_params(2),
        name="cross_attention",
    )(h1, kv, g_x.reshape(1, d), g_ffn.reshape(1, d), wq, wo)


def _top_ranks(scores, n_rounds):
    n = scores.shape[0]
    idx = lax.broadcasted_iota(jnp.int32, scores.shape, 0).astype(F32)
    work = scores
    rank = jnp.full(scores.shape, float(n_rounds), F32)
    tops = []
    for r in range(n_rounds):
        m = jnp.max(work, axis=0, keepdims=True)
        first = jnp.min(jnp.where(work == m, idx, float(n)), axis=0, keepdims=True)
        sel = idx == first
        rank = jnp.where(sel, float(r), rank)
        work = jnp.where(sel, -jnp.inf, work)
        tops.append(m)
    return rank, jnp.concatenate(tops, axis=0)


def _candidate_grid(t1, t2):
    k = PEER_TOPK
    neg_inf = -jnp.inf
    row = lax.broadcasted_iota(jnp.int32, (SUBLANES, t1.shape[1]), 0)
    pieces, spans, start = [], [], 0

    def add(piece, per_i):
        nonlocal start
        pieces.append(piece)
        for off, cnt in per_i:
            spans.append((start + off, start + off + cnt))
        start += piece.shape[0]

    add(t1[0:1] + t2[0:k], [(0, k)])
    add(t1[1:2] + t2[0:8], [(0, 8)])
    for i in (2, 3, 4):
        cnt = k // (i + 1)
        add(jnp.where(row < cnt, t1[i:i + 1] + t2[0:8], neg_inf), [(0, cnt)])
    t1_567 = jnp.where(row < 2, t1[5:6], jnp.where(row < 4, t1[6:7], t1[7:8]))
    t2_01 = jnp.where(row % 2 == 0, t2[0:1], t2[1:2])
    add(jnp.where(row < 6, t1_567 + t2_01, neg_inf), [(0, 2), (2, 2), (4, 2)])
    add(t1[8:k] + t2[0:1], [(r, 1) for r in range(8)])
    assert len(spans) == k and all(b - a == k // (i + 1) for i, (a, b) in enumerate(spans))
    return jnp.concatenate(pieces, axis=0), spans


def _peer_route_kernel(xn_ref, wq_ref, keys_ref, r2_ref, e2_ref, lim_ref, e1_ref, q_scr):
    tile = xn_ref.shape[0]
    kd = LANES
    q = jnp.dot(xn_ref[...], wq_ref[...], preferred_element_type=F32).astype(BF16)
    for hc in range(2 * PEER_HEADS):
        q_scr[hc] = q[:, hc * kd:(hc + 1) * kd]

    k = PEER_TOPK

    def head(h, _):
        for ts in range(tile // LANES):
            tok = slice(ts * LANES, (ts + 1) * LANES)

            def sub_scores(c):
                return lax.dot_general(keys_ref[2 * h + c], q_scr[2 * h + c, tok, :], NT_DIMS,
                                       preferred_element_type=F32)

            rank1, t1 = _top_ranks(sub_scores(0), k)
            rank2, t2 = _top_ranks(sub_scores(1), k)
            cand, spans = _candidate_grid(t1, t2)
            sel_rank, _ = _top_ranks(cand, k)
            sel = sel_rank < float(k)
            top = t1[0:1, :] + t2[0:1, :]
            z = jnp.sum(jnp.where(sel, jnp.exp(cand - top), 0.0), axis=0, keepdims=True)
            sel_f = sel.astype(F32)
            lim = jnp.zeros((PEER_N_KEYS, LANES), F32)
            for i, (a, b) in enumerate(spans):
                row_len = jnp.sum(sel_f[a:b], axis=0, keepdims=True)
                lim = jnp.where(rank1 == float(i), row_len, lim)
            lim_ref[h, :, tok] = lim
            e1_ref[h, :, tok] = jnp.exp(sub_scores(0) - t1[0:1, :]) / z
            e2_ref[h, :, tok] = jnp.exp(sub_scores(1) - t2[0:1, :]).astype(BF16)
            r2_ref[h, :, tok] = rank2.astype(BF16)
        return 0

    lax.fori_loop(0, PEER_HEADS, head, 0)


def _peer_route(xn3, w_query, keys, tile):
    n, d = xn3.shape
    table_spec = pl.BlockSpec((PEER_HEADS, PEER_N_KEYS, tile), lambda i: (0, 0, i))
    return pl.pallas_call(
        _peer_route_kernel,
        grid=(n // tile,),
        in_specs=[pl.BlockSpec((tile, d), lambda i: (i, 0)),
                  pl.BlockSpec(w_query.shape, lambda i: (0, 0)),
                  pl.BlockSpec(keys.shape, lambda i: (0, 0, 0))],
        out_specs=[table_spec] * 4,
        out_shape=[jax.ShapeDtypeStruct((PEER_HEADS, PEER_N_KEYS, n), BF16)] * 2
                  + [jax.ShapeDtypeStruct((PEER_HEADS, PEER_N_KEYS, n), F32)] * 2,
        scratch_shapes=[pltpu.VMEM((2 * PEER_HEADS, tile, LANES), BF16)],
        compiler_params=_params(1),
        name="peer_route",
    )(xn3, w_query, keys)


def _gelu_tanh(x):
    return 0.5 * x * (1.0 + jnp.tanh(math.sqrt(2.0 / math.pi) * (x + 0.044715 * (x * x * x))))


def _peer_dense_kernel(xn_ref, h2_ref, u_ref, vt_ref, r2_ref, e2_ref, lim_ref, e1_ref, gf_ref,
                       o_ref, p_scr, acc_scr):
    j = pl.program_id(1)
    n_exp = u_ref.shape[0]
    tile = xn_ref.shape[0]
    groups = n_exp // PEER_N_KEYS
    assert groups == SUBLANES
    chunk = (PEER_N_KEYS, LANES)

    @pl.when(j == 0)
    def _():
        acc_scr[...] = jnp.zeros_like(acc_scr)

    p_scr[...] = lax.dot_general(u_ref[...], xn_ref[...], NT_DIMS, preferred_element_type=F32).astype(BF16)

    a_rows = pl.ds(pl.multiple_of(j * groups, SUBLANES), SUBLANES)
    for ts in range(tile // LANES):
        tok = slice(ts * LANES, (ts + 1) * LANES)
        lim8 = [lim_ref[h, a_rows, tok] for h in range(PEER_HEADS)]
        e18 = [e1_ref[h, a_rows, tok] for h in range(PEER_HEADS)]
        for g in range(groups):
            rows = slice(g * PEER_N_KEYS, (g + 1) * PEER_N_KEYS)
            gate = jnp.zeros(chunk, BF16)
            for h in range(PEER_HEADS):
                lim_a = jnp.broadcast_to(lim8[h][g:g + 1, :], chunk).astype(BF16)
                e1_a = jnp.broadcast_to(e18[h][g:g + 1, :], chunk).astype(BF16)
                picked = jnp.minimum(jnp.maximum(lim_a - r2_ref[h, :, tok], 0.0), e1_a)
                gate = gate + picked * e2_ref[h, :, tok]
            p_scr[rows, tok] = gate * _gelu_tanh(p_scr[rows, tok])

    acc_scr[...] += jnp.dot(vt_ref[...], p_scr[...], preferred_element_type=F32)

    @pl.when(j == pl.num_programs(1) - 1)
    def _():
        h3 = h2_ref[...] + acc_scr[...].T
        o_ref[...] = _rms(h3, gf_ref[...])


def _peer_dense(xn3, h2, u, vt, tables, g_final, tile, n_exp_blk):
    n, d = xn3.shape
    n_experts = u.shape[0]
    row_spec = pl.BlockSpec((PEER_HEADS, PEER_N_KEYS, tile), lambda i, j: (0, 0, i))
    return pl.pallas_call(
        _peer_dense_kernel,
        grid=(n // tile, n_experts // n_exp_blk),
        in_specs=[pl.BlockSpec((tile, d), lambda i, j: (i, 0)),
                  pl.BlockSpec((tile, d), lambda i, j: (i, 0)),
                  pl.BlockSpec((n_exp_blk, d), lambda i, j: (j, 0)),
                  pl.BlockSpec((d, n_exp_blk), lambda i, j: (0, j)),
                  row_spec, row_spec, row_spec, row_spec,
                  pl.BlockSpec((1, d), lambda i, j: (0, 0))],
        out_specs=pl.BlockSpec((tile, d), lambda i, j: (i, 0)),
        out_shape=jax.ShapeDtypeStruct((n, d), F32),
        scratch_shapes=[pltpu.VMEM((n_exp_blk, tile), BF16),
                        pltpu.VMEM((d, tile), F32)],
        compiler_params=_params(2),
        name="peer_dense",
    )(xn3, h2, u, vt, *tables, g_final.reshape(1, d))


def _tile_sizes(n_tokens, seq):
    return dict(
        proj_tm=min(1024, n_tokens), proj_tn=1024,
        sb_blk_q=min(512, seq), sb_blk_k=min(256, seq),
        mixer_tm=min(256, seq),
        xattn_tm=min(512, seq),
        route_tile=min(512, n_tokens),
        dense_tile=min(512, n_tokens), dense_experts=1024,
    )


def kernel(x, mem, norm_mix, w_in, conv_w, w_branch_attn, w_branch_conv, w_out, norm_xattn, norm_mem,
           xattn_wq, xattn_wk, xattn_wv, xattn_wo, norm_ffn, peer_w_query, peer_sub_keys, peer_u, peer_v,
           norm_final):
    b, s, d = x.shape
    n = b * s
    assert norm_mix.shape[0] == 1, "single-layer block: the final RMSNorm is fused into the last kernel"
    n_sb_heads = w_branch_attn.shape[1] // SB_HEAD_DIM
    sb_width = n_sb_heads * SB_HEAD_DIM
    ts = _tile_sizes(n, s)
    h = x.reshape(n, d)
    mem2d = mem.reshape(-1, d)

    w_in_bf = w_in[0].astype(BF16)
    qkv = _norm_matmul(h, norm_mix[0], w_in_bf[:, :3 * sb_width], BF16, ts["proj_tm"], ts["proj_tn"])
    rest = _norm_matmul(h, norm_mix[0], w_in_bf[:, 3 * sb_width:], F32, ts["proj_tm"], ts["proj_tn"])
    y_attn = _sb_attention(qkv.reshape(b, s, 3 * sb_width), n_sb_heads, ts["sb_blk_q"], ts["sb_blk_k"])
    h1 = _mixer_out(rest, y_attn.reshape(n, sb_width), h, conv_w[0],
                    w_branch_attn[0].astype(BF16), w_branch_conv[0].astype(BF16), w_out[0].astype(BF16),
                    b, ts["mixer_tm"])

    w_kv = jnp.concatenate([xattn_wk[0], xattn_wv[0]], axis=1).astype(BF16)
    kv = _norm_matmul(mem2d, norm_mem[0], w_kv, BF16, min(1024, mem2d.shape[0]), ts["proj_tn"])
    h2, xn3 = _cross_attention(h1, kv.reshape(b, -1, 2 * d), norm_xattn[0], norm_ffn[0],
                               xattn_wq[0].astype(BF16), xattn_wo[0].astype(BF16), b, ts["xattn_tm"])

    keys = peer_sub_keys[0].reshape(2 * PEER_HEADS, PEER_N_KEYS, -1).astype(BF16)
    tables = _peer_route(xn3, peer_w_query[0].astype(BF16), keys, ts["route_tile"])
    out = _peer_dense(xn3, h2, peer_u[0].astype(BF16), peer_v[0].astype(BF16).T, tables, norm_final,
                      ts["dense_tile"], ts["dense_experts"])
    return out.reshape(b, s, d)
```
